```python
import math
import jax, jax.numpy as jnp
from jax import lax
import numpy as np

D_MODEL = 1024
BATCH = 32
SEQ = 2048
DEPTH = 2

D_MIX = D_MODEL
D_A = D_MIX // 2
D_B = D_MIX - D_A
N_HEADS_A = 8
HEAD_DIM_A = D_A // N_HEADS_A
N_GROUPS_B = 8
CHUNK = 128
CONV_WIDTH = 3
D_FF = 2816
N_MOD = 9
D_IN_PROJ = 2 * D_A + 3 * D_B
EPS = 1e-6

kernel_name = "hybrid_sgu_shortconv_macaron_adaln"


def rms_norm(x, g):
    xf = x.astype(jnp.float32)
    y = xf * lax.rsqrt(jnp.mean(xf * xf, axis=-1, keepdims=True) + EPS)
    return (y * g.astype(jnp.float32)).astype(x.dtype)


def layer_norm(x, g, b):
    xf = x.astype(jnp.float32)
    mu = jnp.mean(xf, axis=-1, keepdims=True)
    var = jnp.mean(jnp.square(xf - mu), axis=-1, keepdims=True)
    y = (xf - mu) * lax.rsqrt(var + EPS)
    return (y * g.astype(jnp.float32) + b.astype(jnp.float32)).astype(x.dtype)


def modulate(h, shift, scale):
    return h * (1 + scale[:, None, :]) + shift[:, None, :]


def swiglu_ffn(h, w_gu, w_down):
    gu = jnp.einsum('bsd,df->bsf', h, w_gu)
    g, u = jnp.split(gu, 2, axis=-1)
    return jnp.einsum('bsf,fd->bsd', jax.nn.silu(g) * u, w_down)


def chunked_sgu(u, v, ln_g, ln_b, w_s, b_s):
    bsz, s, _ = v.shape
    n_chunks = s // CHUNK
    v = layer_norm(v.reshape(bsz, s, N_HEADS_A, HEAD_DIM_A), ln_g, ln_b)
    v = v.reshape(bsz, n_chunks, CHUNK, N_HEADS_A, HEAD_DIM_A)
    causal = jnp.tril(jnp.ones((CHUNK, CHUNK), dtype=bool))
    w_masked = jnp.where(causal[None], w_s, jnp.zeros_like(w_s))
    mixed = jnp.einsum('hts,bcshd->bcthd', w_masked, v)
    mixed = mixed + jnp.transpose(b_s)[None, None, :, :, None]
    return u * mixed.reshape(bsz, s, D_A)


def short_gated_conv(b_gate, c_gate, xb, conv_w):
    s = xb.shape[1]
    z = c_gate * xb
    zp = jnp.pad(z, ((0, 0), (CONV_WIDTH - 1, 0), (0, 0)))
    conv = zp[:, 0:s] * conv_w[0] + zp[:, 1:s + 1] * conv_w[1] + zp[:, 2:s + 2] * conv_w[2]
    return b_gate * conv


def setup_inputs(seed: int = 0) -> dict:
    key = jax.random.key(seed)
    ks = jax.random.split(key, 24)

    def nrm(k, shape, scale):
        return scale * jax.random.normal(k, shape, jnp.float32)

    def gain(k, shape):
        return 1.0 + 0.02 * jax.random.normal(k, shape, jnp.float32)

    return {
        "x": nrm(ks[0], (BATCH, SEQ, D_MODEL), 1.0),
        "c": nrm(ks[1], (BATCH, D_MODEL), 1.0),
        "ada_w": nrm(ks[2], (DEPTH, D_MODEL, N_MOD * D_MODEL), 0.5 * D_MODEL ** -0.5),
        "ada_b": nrm(ks[3], (DEPTH, N_MOD * D_MODEL), 0.01),
        "norm_ffn1_g": gain(ks[4], (DEPTH, D_MODEL)),
        "ffn1_w_gu": nrm(ks[5], (DEPTH, D_MODEL, 2 * D_FF), D_MODEL ** -0.5),
        "ffn1_w_down": nrm(ks[6], (DEPTH, D_FF, D_MODEL), D_FF ** -0.5),
        "norm_mix_g": gain(ks[7], (DEPTH, D_MODEL)),
        "mix_w_in": nrm(ks[8], (DEPTH, D_MODEL, D_IN_PROJ), D_MODEL ** -0.5),
        "sgu_ln_g": gain(ks[9], (DEPTH, HEAD_DIM_A)),
        "sgu_ln_b": nrm(ks[10], (DEPTH, HEAD_DIM_A), 0.02),
        "sgu_w_s": nrm(ks[11], (DEPTH, N_HEADS_A, CHUNK, CHUNK), CHUNK ** -0.5),
        "sgu_b": gain(ks[12], (DEPTH, N_HEADS_A, CHUNK)),
        "conv_w": nrm(ks[13], (DEPTH, CONV_WIDTH, D_B), CONV_WIDTH ** -0.5),
        "out_norm_g": gain(ks[14], (DEPTH, D_MIX)),
        "mix_w_out": nrm(ks[15], (DEPTH, D_MIX, D_MODEL), D_MIX ** -0.5),
        "norm_ffn2_g": gain(ks[16], (DEPTH, D_MODEL)),
        "ffn2_w_gu": nrm(ks[17], (DEPTH, D_MODEL, 2 * D_FF), D_MODEL ** -0.5),
        "ffn2_w_down": nrm(ks[18], (DEPTH, D_FF, D_MODEL), D_FF ** -0.5),
        "final_norm_g": gain(ks[19], (D_MODEL,)),
    }


def reference(x, c, ada_w, ada_b, norm_ffn1_g, ffn1_w_gu, ffn1_w_down, norm_mix_g,
              mix_w_in, sgu_ln_g, sgu_ln_b, sgu_w_s, sgu_b, conv_w, out_norm_g,
              mix_w_out, norm_ffn2_g, ffn2_w_gu, ffn2_w_down, final_norm_g):
    c_act = jax.nn.silu(c)
    split_points = [D_A, 2 * D_A, 2 * D_A + D_B, 2 * D_A + 2 * D_B]
    for l in range(DEPTH):
        ada = jnp.einsum('bd,de->be', c_act, ada_w[l]) + ada_b[l]
        (sh1, sc1, g1, sh2, sc2, g2, sh3, sc3, g3) = jnp.split(ada, N_MOD, axis=-1)

        h = modulate(rms_norm(x, norm_ffn1_g[l]), sh1, sc1)
        x = x + 0.5 * g1[:, None, :] * swiglu_ffn(h, ffn1_w_gu[l], ffn1_w_down[l])

        h = modulate(rms_norm(x, norm_mix_g[l]), sh2, sc2)
        proj = jnp.einsum('bsd,de->bse', h, mix_w_in[l])
        u_a, v_a, b_gate, c_gate, xb = jnp.split(proj, split_points, axis=-1)
        y_a = chunked_sgu(jax.nn.gelu(u_a, approximate=False), jax.nn.gelu(v_a, approximate=False),
                          sgu_ln_g[l], sgu_ln_b[l], sgu_w_s[l], sgu_b[l])
        y_b = short_gated_conv(b_gate, c_gate, xb, conv_w[l])
        y = jnp.concatenate([rms_norm(y_a, out_norm_g[l, :D_A]),
                             rms_norm(y_b, out_norm_g[l, D_A:])], axis=-1)
        x = x + g2[:, None, :] * jnp.einsum('bse,ed->bsd', y, mix_w_out[l])

        h = modulate(rms_norm(x, norm_ffn2_g[l]), sh3, sc3)
        x = x + 0.5 * g3[:, None, :] * swiglu_ffn(h, ffn2_w_gu[l], ffn2_w_down[l])
    return rms_norm(x, final_norm_g)
```

```python
import functools

import jax
import jax.numpy as jnp
from jax import lax
from jax.experimental import pallas as pl
from jax.experimental.pallas import tpu as pltpu

D_MODEL = 1024
D_A = D_MODEL // 2
D_B = D_MODEL - D_A
N_HEADS_A = 8
HEAD_DIM_A = D_A // N_HEADS_A
CHUNK = 128
CONV_WIDTH = 3
D_FF = 2816
N_MOD = 9
EPS = 1e-6

_F32 = jnp.float32
_BF16 = jnp.bfloat16

_V7X_MXU_DIM = 256
_V7X_F32_SUBLANES = 8
_MIB = 1024 * 1024

_FFN_TM = 512
_MIX_TM = 512
_FFN_FC = _V7X_MXU_DIM
_FFN_NCH = D_FF // _FFN_FC
assert _FFN_NCH * _FFN_FC == D_FF
_ADA_TN = 1536
assert (N_MOD * D_MODEL) % _ADA_TN == 0
_HEAD_PAIRS = N_HEADS_A // 2
assert 2 * CHUNK == _V7X_MXU_DIM

_FFN_VMEM_BYTES = 52 * _MIB
_MIX_VMEM_BYTES = 48 * _MIB
_ADA_VMEM_BYTES = 32 * _MIB


def _resident(shape):
  return pl.BlockSpec(shape, lambda *_: (0,) * len(shape), pipeline_mode=pl.Buffered(1))


def _gelu_exact(x):
  return 0.5 * x * (1.0 + lax.erf(x * (2.0 ** -0.5)))


def _rms_modulate(xb, g, shift, scale):
  y = xb * lax.rsqrt(jnp.mean(xb * xb, axis=-1, keepdims=True) + EPS)
  return (y * g) * (1.0 + scale) + shift


def _ada_kernel(c_ref, w_ref, b_ref, o_ref):
  c = c_ref[...]
  c_act = (c * jax.nn.sigmoid(c)).astype(_BF16)
  w = w_ref[0].astype(_BF16)
  o_ref[0] = jnp.dot(c_act, w, preferred_element_type=_F32) + b_ref[0]


def _ada_call(c, ada_w, ada_b):
  depth, d, e = ada_w.shape
  bsz = c.shape[0]
  return pl.pallas_call(
      _ada_kernel,
      grid=(depth, e // _ADA_TN),
      in_specs=[
          pl.BlockSpec((bsz, d), lambda l, j: (0, 0)),
          pl.BlockSpec((1, d, _ADA_TN), lambda l, j: (l, 0, j)),
          pl.BlockSpec((1, 1, _ADA_TN), lambda l, j: (l, 0, j)),
      ],
      out_specs=pl.BlockSpec((1, bsz, _ADA_TN), lambda l, j: (l, 0, j)),
      out_shape=jax.ShapeDtypeStruct((depth, bsz, e), _F32),
      compiler_params=pltpu.CompilerParams(
          dimension_semantics=("parallel", "parallel"), vmem_limit_bytes=_ADA_VMEM_BYTES),
      name="ada_proj",
  )(c, ada_w, ada_b.reshape(depth, 1, e))


def _ffn_kernel(*refs, mod_base, final_norm):
  if final_norm:
    x_ref, mod_ref, g_ref, wgu_ref, wd_ref, fg_ref, o_ref, a_ref = refs
  else:
    x_ref, mod_ref, g_ref, wgu_ref, wd_ref, o_ref, a_ref = refs
  xb = x_ref[0]
  mod = mod_ref[0, 0]
  shift = mod[mod_base:mod_base + 1]
  scale = mod[mod_base + 1:mod_base + 2]
  gate = mod[mod_base + 2:mod_base + 3]
  h = _rms_modulate(xb, g_ref[...], shift, scale).astype(_BF16)
  for j in range(_FFN_NCH):
    gu = jnp.dot(h, wgu_ref[j], preferred_element_type=_F32)
    g = gu[:, :_FFN_FC]
    u = gu[:, _FFN_FC:]
    a_ref[:, j * _FFN_FC:(j + 1) * _FFN_FC] = ((g * jax.nn.sigmoid(g)) * u).astype(_BF16)
  down = jnp.dot(a_ref[...], wd_ref[...], preferred_element_type=_F32)
  y = xb + (0.5 * gate) * down
  if final_norm:
    y = (y * lax.rsqrt(jnp.mean(y * y, axis=-1, keepdims=True) + EPS)) * fg_ref[...]
  o_ref[0] = y


def _ffn_call(x, ada4, layer, mod_base, norm_g, wgu_r, wd, final_g=None):
  bsz, s, d = x.shape
  tm = _FFN_TM
  final_norm = final_g is not None
  in_specs = [
      pl.BlockSpec((1, tm, d), lambda b, i: (b, i, 0)),
      pl.BlockSpec((1, 1, N_MOD, d), lambda b, i: (layer, b, 0, 0)),
      _resident((1, d)),
      _resident(wgu_r.shape),
      _resident(wd.shape),
  ]
  args = [x, ada4, norm_g.reshape(1, d), wgu_r, wd]
  if final_norm:
    in_specs.append(_resident((1, d)))
    args.append(final_g.reshape(1, d))
  return pl.pallas_call(
      functools.partial(_ffn_kernel, mod_base=mod_base, final_norm=final_norm),
      grid=(bsz, s // tm),
      in_specs=in_specs,
      out_specs=pl.BlockSpec((1, tm, d), lambda b, i: (b, i, 0)),
      out_shape=jax.ShapeDtypeStruct(x.shape, _F32),
      scratch_shapes=[pltpu.VMEM((tm, D_FF), _BF16)],
      compiler_params=pltpu.CompilerParams(
          dimension_semantics=("parallel", "parallel"), vmem_limit_bytes=_FFN_VMEM_BYTES),
      name="ffn_final" if final_norm else "ffn",
  )(*args)


def _mix_kernel(x_ref, mod_ref, g_ref, wuvt_ref, wb_ref, lng_ref, lnb_ref, wbd_ref, bs_ref,
                cw_ref, ga_ref, gb_ref, woa_ref, wob_ref, o_ref, zs_ref, *, tm):
  nc = tm // CHUNK
  xb = x_ref[0]
  mod = mod_ref[0, 0]
  h = _rms_modulate(xb, g_ref[...], mod[3:4], mod[4:5]).astype(_BF16)
  gate = mod[5:6]

  uvt = lax.dot_general(wuvt_ref[...], h, (((1,), (1,)), ((), ())),
                        preferred_element_type=_F32)
  ut = _gelu_exact(uvt[:D_A])
  vt = _gelu_exact(uvt[D_A:]).reshape(N_HEADS_A, HEAD_DIM_A, tm)
  mu = jnp.mean(vt, axis=1, keepdims=True)
  dv = vt - mu
  var = jnp.mean(dv * dv, axis=1, keepdims=True)
  vn = ((dv * lax.rsqrt(var + EPS)) * lng_ref[...] + lnb_ref[...]).astype(_BF16)

  row = lax.broadcasted_iota(jnp.int32, (2 * CHUNK, 2 * CHUNK), 0)
  col = lax.broadcasted_iota(jnp.int32, (2 * CHUNK, 2 * CHUNK), 1)
  keep = ((row // CHUNK) == (col // CHUNK)) & ((col % CHUNK) >= (row % CHUNK))
  heads = [None] * N_HEADS_A
  for p in range(_HEAD_PAIRS):
    w = jnp.where(keep, wbd_ref[p], 0.0).astype(_BF16)
    lhs = jnp.concatenate(
        [jnp.concatenate([vn[2 * p][:, c * CHUNK:(c + 1) * CHUNK],
                          vn[2 * p + 1][:, c * CHUNK:(c + 1) * CHUNK]], axis=1)
         for c in range(nc)], axis=0)
    res = jnp.dot(lhs, w, preferred_element_type=_F32)
    for hh in range(2):
      head = 2 * p + hh
      bias = bs_ref[head:head + 1, :]
      heads[head] = jnp.concatenate(
          [res[c * HEAD_DIM_A:(c + 1) * HEAD_DIM_A, hh * CHUNK:(hh + 1) * CHUNK] + bias
           for c in range(nc)], axis=1)
  ya_t = ut * jnp.concatenate(heads, axis=0)
  ya_t = (ya_t * lax.rsqrt(jnp.mean(ya_t * ya_t, axis=0, keepdims=True) + EPS)) * ga_ref[...]
  out = lax.dot_general(ya_t.astype(_BF16), woa_ref[...], (((0,), (0,)), ((), ())),
                        preferred_element_type=_F32)

  pb = jnp.dot(h, wb_ref[...], preferred_element_type=_F32)
  z = pb[:, D_B:2 * D_B] * pb[:, 2 * D_B:]
  halo = _V7X_F32_SUBLANES

  @pl.when(pl.program_id(1) == 0)
  def _():
    zs_ref[0:halo, :] = jnp.zeros((halo, D_B), _F32)

  zs_ref[halo:halo + tm, :] = z
  cw = cw_ref[...]
  conv = (zs_ref[halo - 2:halo - 2 + tm, :] * cw[0:1]
          + zs_ref[halo - 1:halo - 1 + tm, :] * cw[1:2]
          + z * cw[2:3])
  zs_ref[0:halo, :] = z[tm - halo:, :]
  yb = pb[:, :D_B] * conv
  yb = (yb * lax.rsqrt(jnp.mean(yb * yb, axis=-1, keepdims=True) + EPS)) * gb_ref[...]
  out = out + jnp.dot(yb.astype(_BF16), wob_ref[...], preferred_element_type=_F32)

  o_ref[0] = xb + gate * out


def _mix_call(x, ada4, layer, norm_g, wuvt, wb, ln_g, ln_b, wbd, b_s, conv_w, out_g, woa, wob):
  bsz, s, d = x.shape
  tm = _MIX_TM
  args = [
      x, ada4, norm_g.reshape(1, d), wuvt, wb,
      ln_g.reshape(HEAD_DIM_A, 1), ln_b.reshape(HEAD_DIM_A, 1), wbd, b_s, conv_w,
      out_g[:D_A].reshape(D_A, 1), out_g[D_A:].reshape(1, D_B), woa, wob,
  ]
  in_specs = [
      pl.BlockSpec((1, tm, d), lambda b, i: (b, i, 0)),
      pl.BlockSpec((1, 1, N_MOD, d), lambda b, i: (layer, b, 0, 0)),
  ] + [_resident(a.shape) for a in args[2:]]
  return pl.pallas_call(
      functools.partial(_mix_kernel, tm=tm),
      grid=(bsz, s // tm),
      in_specs=in_specs,
      out_specs=pl.BlockSpec((1, tm, d), lambda b, i: (b, i, 0)),
      out_shape=jax.ShapeDtypeStruct(x.shape, _F32),
      scratch_shapes=[pltpu.VMEM((tm + _V7X_F32_SUBLANES, D_B), _F32)],
      compiler_params=pltpu.CompilerParams(
          dimension_semantics=("arbitrary", "arbitrary"), vmem_limit_bytes=_MIX_VMEM_BYTES),
      name="mix",
  )(*args)


def _swiglu_in_layout(w_gu):
  d = w_gu.shape[0]
  g = w_gu[:, :D_FF].reshape(d, _FFN_NCH, _FFN_FC)
  u = w_gu[:, D_FF:].reshape(d, _FFN_NCH, _FFN_FC)
  return jnp.concatenate([g, u], axis=2).transpose(1, 0, 2).astype(_BF16)


def _sgu_pair_layout(w_s):
  wt = jnp.swapaxes(w_s, 1, 2).reshape(_HEAD_PAIRS, 2, CHUNK, CHUNK)
  zero = jnp.zeros((_HEAD_PAIRS, CHUNK, CHUNK), w_s.dtype)
  top = jnp.concatenate([wt[:, 0], zero], axis=2)
  bot = jnp.concatenate([zero, wt[:, 1]], axis=2)
  return jnp.concatenate([top, bot], axis=1)


def kernel(x, c, ada_w, ada_b, norm_ffn1_g, ffn1_w_gu, ffn1_w_down, norm_mix_g, mix_w_in,
           sgu_ln_g, sgu_ln_b, sgu_w_s, sgu_b, conv_w, out_norm_g, mix_w_out, norm_ffn2_g,
           ffn2_w_gu, ffn2_w_down, final_norm_g):
  depth = ada_w.shape[0]
  bsz = x.shape[0]
  assert x.shape[1] % _FFN_TM == 0 and x.shape[1] % _MIX_TM == 0 and _MIX_TM % CHUNK == 0
  ada4 = _ada_call(c, ada_w, ada_b).reshape(depth, bsz, N_MOD, D_MODEL)
  for l in range(depth):
    x = _ffn_call(x, ada4, l, 0, norm_ffn1_g[l], _swiglu_in_layout(ffn1_w_gu[l]),
                  ffn1_w_down[l].astype(_BF16))
    w_in = mix_w_in[l]
    x = _mix_call(
        x, ada4, l, norm_mix_g[l],
        w_in[:, :2 * D_A].T.astype(_BF16), w_in[:, 2 * D_A:].astype(_BF16),
        sgu_ln_g[l], sgu_ln_b[l], _sgu_pair_layout(sgu_w_s[l]), sgu_b[l], conv_w[l],
        out_norm_g[l], mix_w_out[l, :D_A].astype(_BF16), mix_w_out[l, D_A:].astype(_BF16))
    x = _ffn_call(x, ada4, l, 6, norm_ffn2_g[l], _swiglu_in_layout(ffn2_w_gu[l]),
                  ffn2_w_down[l].astype(_BF16),
                  final_g=final_norm_g if l == depth - 1 else None)
  return x
```
